```python
import jax, jax.numpy as jnp
from jax import lax
import numpy as np


D_MODEL = 4096
BATCH = 4
SEQ = 2048
DEPTH = 1
DEC_BATCH = 32
DEC_SEQ = 1
PAST_LEN = 8192
PAGE_SIZE = 128

N_HEADS = 16
HEAD_DIM = 128
ATTN_WIDTH = N_HEADS * HEAD_DIM
POOL_WIDTH = D_MODEL - ATTN_WIDTH
POOL_GROUPS = 4
POOL_GROUP_DIM = POOL_WIDTH // POOL_GROUPS
POOL_WINDOWS = (2, 4, 8, 16)
POOL_MAX = 16
DILATED_PATTERNS = ((128, 1), (512, 4), (2048, 16))
WIN_MAX = 2048
D_FF = 11008
N_ADA = 9
ALPHA = (2 * DEPTH) ** 0.25
BETA = (8 * DEPTH) ** -0.25
LN_EPS = 1e-5

kernel_name = 'hymba_dilated_pool_macaron_step'


def layer_norm(x, g, b):
    xf = x.astype(jnp.float32)
    mu = jnp.mean(xf, axis=-1, keepdims=True)
    var = jnp.mean(jnp.square(xf - mu), axis=-1, keepdims=True)
    y = (xf - mu) * lax.rsqrt(var + LN_EPS) * g.astype(jnp.float32) + b.astype(jnp.float32)
    return y.astype(x.dtype)


def ada_terms(c, w_ada, b_ada):
    ada = jax.nn.silu(c) @ w_ada + b_ada
    return ada.reshape(c.shape[0], 1, N_ADA, D_MODEL)


def swiglu_half_step(x, ada, i, w_gate, w_up, w_down, g, b):
    u = x * (1 + ada[:, :, i + 1]) + ada[:, :, i]
    h = jax.nn.silu(u @ w_gate) * (u @ w_up)
    out = 0.5 * (1 + ada[:, :, i + 2]) * (h @ w_down)
    return layer_norm(ALPHA * x + out, g, b)


def mixer_proj(x, ada, w_in):
    B, L, _ = x.shape
    u = x * (1 + ada[:, :, 4]) + ada[:, :, 3]
    proj = u @ w_in
    q = proj[..., :ATTN_WIDTH].reshape(B, L, N_HEADS, HEAD_DIM)
    k = proj[..., ATTN_WIDTH:2 * ATTN_WIDTH].reshape(B, L, N_HEADS, HEAD_DIM)
    v = proj[..., 2 * ATTN_WIDTH:3 * ATTN_WIDTH].reshape(B, L, N_HEADS, HEAD_DIM)
    p = proj[..., 3 * ATTN_WIDTH:]
    return q, k, v, p


def dilated_band_attention(q, k, v, window, dilation):
    B, S, H, Dh = q.shape
    nb = window // dilation
    unit = dilation * nb
    s_pad = -(-S // unit) * unit
    L = s_pad // dilation
    nblk = L // nb
    pad = ((0, 0), (0, s_pad - S), (0, 0), (0, 0))

    def split(a):
        a = jnp.pad(a, pad).reshape(B, L, dilation, H, Dh).transpose(0, 2, 1, 3, 4)
        return a.reshape(B, dilation, nblk, nb, H, Dh)

    def with_prev(a):
        prev = jnp.pad(a[:, :, :-1], ((0, 0), (0, 0), (1, 0), (0, 0), (0, 0), (0, 0)))
        return jnp.concatenate([prev, a], axis=3)

    qs = split(q)
    kk = with_prev(split(k))
    vv = with_prev(split(v))
    s = jnp.einsum('brnqhd,brnkhd->brnhqk', qs, kk, preferred_element_type=jnp.float32) * (Dh ** -0.5)
    ql = jnp.arange(nb)[:, None]
    kl = jnp.arange(2 * nb)[None, :]
    dist = ql + nb - kl
    band = (dist >= 0) & (dist <= nb)
    key_sub = jnp.arange(nblk)[:, None, None] * nb + kl[None] - nb
    mask = band[None] & (key_sub >= 0)
    s = jnp.where(mask[None, None, :, None], s, -jnp.inf)
    lse = jax.nn.logsumexp(s, axis=-1)
    p = jnp.exp(s - lse[..., None])
    o = jnp.einsum('brnhqk,brnkhd->brnqhd', p.astype(v.dtype), vv, preferred_element_type=jnp.float32)
    o = o.reshape(B, dilation, L, H, Dh).transpose(0, 2, 1, 3, 4).reshape(B, s_pad, H, Dh)[:, :S]
    lse = lse.transpose(0, 1, 2, 4, 3).reshape(B, dilation, L, H).transpose(0, 2, 1, 3).reshape(B, s_pad, H)[:, :S]
    return o, lse


def dilated_attention_step(q, kc, vc, window, dilation, n_buf):
    T = q.shape[1]
    nb = window // dilation
    qidx = n_buf + jnp.arange(T)
    kidx = qidx[:, None] - dilation * jnp.arange(nb + 1)[None, :]
    valid = kidx >= 0
    kidx = jnp.clip(kidx, 0)
    kg = kc[:, kidx]
    vg = vc[:, kidx]
    s = jnp.einsum('bthd,btjhd->bhtj', q, kg, preferred_element_type=jnp.float32) * (HEAD_DIM ** -0.5)
    s = jnp.where(valid[None, None], s, -jnp.inf)
    lse = jax.nn.logsumexp(s, axis=-1)
    p = jnp.exp(s - lse[..., None])
    o = jnp.einsum('bhtj,btjhd->bthd', p.astype(vc.dtype), vg, preferred_element_type=jnp.float32)
    return o, lse.transpose(0, 2, 1)


def combine_patterns(outs, lses):
    w = jax.nn.softmax(jnp.stack(lses, axis=0), axis=0)
    o = jnp.stack(outs, axis=0)
    return jnp.sum(w[..., None] * o, axis=0)


def multiscale_pool(u):
    B, L, _ = u.shape
    ug = u.reshape(B, L, POOL_GROUPS, POOL_GROUP_DIM).astype(jnp.float32)
    cs = jnp.pad(jnp.cumsum(ug, axis=1), ((0, 0), (1, 0), (0, 0), (0, 0)))
    idx = jnp.arange(L)
    outs = []
    for g, w in enumerate(POOL_WINDOWS):
        lo = jnp.maximum(idx - w + 1, 0)
        cnt = (idx + 1 - lo).astype(jnp.float32)
        outs.append((cs[:, idx + 1, g] - cs[:, lo, g]) / cnt[None, :, None])
    return jnp.stack(outs, axis=2) - ug


def mixer_out(attn_o, pool_d, ada, pool_w, pool_scale, w_out, dtype):
    B, L = attn_o.shape[:2]
    pool_y = jnp.einsum('blgc,gcd->blgd', pool_d.astype(dtype), pool_w).reshape(B, L, POOL_WIDTH) * pool_scale
    cat = jnp.concatenate([attn_o.reshape(B, L, ATTN_WIDTH).astype(dtype), pool_y.astype(dtype)], axis=-1)
    return (1 + ada[:, :, 5]) * (cat @ w_out)


def setup_inputs(seed: int = 0) -> dict:
    key = jax.random.key(seed)
    ks = jax.random.split(key, 24)
    f32 = jnp.float32
    wb = min(WIN_MAX, PAST_LEN)
    pb = min(POOL_MAX - 1, PAST_LEN)
    nrm = lambda k, shape, s: jax.random.normal(k, shape, f32) * s
    w_in_qk = nrm(ks[0], (DEPTH, D_MODEL, 2 * ATTN_WIDTH), D_MODEL ** -0.5)
    w_in_v = nrm(ks[1], (DEPTH, D_MODEL, ATTN_WIDTH), BETA * D_MODEL ** -0.5)
    w_in_p = nrm(ks[2], (DEPTH, D_MODEL, POOL_WIDTH), D_MODEL ** -0.5)
    return {
        'x_prompt': nrm(ks[3], (BATCH, SEQ, D_MODEL), 1.0),
        'x_sample': nrm(ks[4], (DEC_BATCH, DEC_SEQ, D_MODEL), 1.0),
        'cache_k': nrm(ks[5], (DEPTH, DEC_BATCH, wb, N_HEADS, HEAD_DIM), 1.0),
        'cache_v': nrm(ks[6], (DEPTH, DEC_BATCH, wb, N_HEADS, HEAD_DIM), BETA),
        'state_pool': nrm(ks[7], (DEPTH, DEC_BATCH, pb, POOL_WIDTH), 1.0),
        'c_prompt': nrm(ks[8], (BATCH, D_MODEL), 1.0),
        'c_sample': nrm(ks[9], (DEC_BATCH, D_MODEL), 1.0),
        'w_ada': nrm(ks[10], (DEPTH, D_MODEL, N_ADA * D_MODEL), 0.2 * D_MODEL ** -0.5),
        'b_ada': nrm(ks[11], (DEPTH, N_ADA * D_MODEL), 0.02),
        'ffn1_w_gate': nrm(ks[12], (DEPTH, D_MODEL, D_FF), D_MODEL ** -0.5),
        'ffn1_w_up': nrm(ks[13], (DEPTH, D_MODEL, D_FF), D_MODEL ** -0.5),
        'ffn1_w_down': nrm(ks[14], (DEPTH, D_FF, D_MODEL), BETA * D_FF ** -0.5),
        'ln1_g': 1.0 + nrm(ks[15], (DEPTH, D_MODEL), 0.02),
        'ln1_b': nrm(ks[16], (DEPTH, D_MODEL), 0.02),
        'w_in': jnp.concatenate([w_in_qk, w_in_v, w_in_p], axis=-1),
        'pool_w': nrm(ks[17], (DEPTH, POOL_GROUPS, POOL_GROUP_DIM, POOL_GROUP_DIM), POOL_GROUP_DIM ** -0.5),
        'pool_scale': 1.0 + nrm(ks[18], (DEPTH, POOL_WIDTH), 0.1),
        'w_out': nrm(ks[19], (DEPTH, D_MODEL, D_MODEL), BETA * D_MODEL ** -0.5),
        'ln2_g': 1.0 + nrm(ks[20], (DEPTH, D_MODEL), 0.02),
        'ln2_b': nrm(ks[21], (DEPTH, D_MODEL), 0.02),
        'ffn2_w_gate': nrm(ks[22], (DEPTH, D_MODEL, D_FF), D_MODEL ** -0.5),
        'ffn2_w_up': nrm(ks[23], (DEPTH, D_MODEL, D_FF), D_MODEL ** -0.5),
        'ffn2_w_down': nrm(jax.random.fold_in(key, 101), (DEPTH, D_FF, D_MODEL), BETA * D_FF ** -0.5),
        'ln3_g': 1.0 + nrm(jax.random.fold_in(key, 102), (DEPTH, D_MODEL), 0.02),
        'ln3_b': nrm(jax.random.fold_in(key, 103), (DEPTH, D_MODEL), 0.02),
    }


def reference(x_prompt, x_sample, cache_k, cache_v, state_pool, c_prompt, c_sample,
              w_ada, b_ada, ffn1_w_gate, ffn1_w_up, ffn1_w_down, ln1_g, ln1_b,
              w_in, pool_w, pool_scale, w_out, ln2_g, ln2_b,
              ffn2_w_gate, ffn2_w_up, ffn2_w_down, ln3_g, ln3_b):
    S = x_prompt.shape[1]
    T = x_sample.shape[1]
    wb = cache_k.shape[2]
    keep_win = min(WIN_MAX, S)
    keep_pool = min(POOL_MAX - 1, S)
    xp, xs = x_prompt, x_sample
    kp_l, vp_l, pp_l, ks_l, vs_l, ps_l = [], [], [], [], [], []
    for l in range(DEPTH):
        ada_p = ada_terms(c_prompt, w_ada[l], b_ada[l])
        ada_s = ada_terms(c_sample, w_ada[l], b_ada[l])
        xp = swiglu_half_step(xp, ada_p, 0, ffn1_w_gate[l], ffn1_w_up[l], ffn1_w_down[l], ln1_g[l], ln1_b[l])
        xs = swiglu_half_step(xs, ada_s, 0, ffn1_w_gate[l], ffn1_w_up[l], ffn1_w_down[l], ln1_g[l], ln1_b[l])

        q, k, v, up = mixer_proj(xp, ada_p, w_in[l])
        outs, lses = [], []
        for (win, dil) in DILATED_PATTERNS:
            o, lse = dilated_band_attention(q, k, v, win, dil)
            outs.append(o)
            lses.append(lse)
        attn = combine_patterns(outs, lses)
        pool_d = multiscale_pool(up)
        out = mixer_out(attn, pool_d, ada_p, pool_w[l], pool_scale[l], w_out[l], xp.dtype)
        xp = layer_norm(ALPHA * xp + out, ln2_g[l], ln2_b[l])
        kp_l.append(k[:, S - keep_win:])
        vp_l.append(v[:, S - keep_win:])
        pp_l.append(up[:, S - keep_pool:])

        q, k, v, us = mixer_proj(xs, ada_s, w_in[l])
        kc = jnp.concatenate([cache_k[l].astype(k.dtype), k], axis=1)
        vc = jnp.concatenate([cache_v[l].astype(v.dtype), v], axis=1)
        outs, lses = [], []
        for (win, dil) in DILATED_PATTERNS:
            o, lse = dilated_attention_step(q, kc, vc, win, dil, wb)
            outs.append(o)
            lses.append(lse)
        attn = combine_patterns(outs, lses)
        pc = jnp.concatenate([state_pool[l].astype(us.dtype), us], axis=1)
        pool_d = multiscale_pool(pc)[:, pc.shape[1] - T:]
        out = mixer_out(attn, pool_d, ada_s, pool_w[l], pool_scale[l], w_out[l], xs.dtype)
        xs = layer_norm(ALPHA * xs + out, ln2_g[l], ln2_b[l])
        ks_l.append(k)
        vs_l.append(v)
        ps_l.append(us)

        xp = swiglu_half_step(xp, ada_p, 6, ffn2_w_gate[l], ffn2_w_up[l], ffn2_w_down[l], ln3_g[l], ln3_b[l])
        xs = swiglu_half_step(xs, ada_s, 6, ffn2_w_gate[l], ffn2_w_up[l], ffn2_w_down[l], ln3_g[l], ln3_b[l])
    return (xp, xs, jnp.stack(kp_l), jnp.stack(vp_l), jnp.stack(pp_l), jnp.stack(ks_l), jnp.stack(vs_l), jnp.stack(ps_l))
```

```python
import functools
import math

import numpy as np
import jax
import jax.numpy as jnp
from jax import lax
from jax.experimental import pallas as pl
from jax.experimental.pallas import tpu as pltpu

F32 = jnp.float32
BF16 = jnp.bfloat16

V7X_SCOPED_VMEM_BYTES = 60000 * 1024
LANE = 128
MXU_DIM = 256

LN_EPS = 1e-5
N_ADA = 9
POOL_WINDOWS = (2, 4, 8, 16)
POOL_MAX = 16
DILATED_PATTERNS = ((128, 1), (512, 4), (2048, 16))
WIN_MAX = 2048
ATTN_BLOCK = 256


def _params(semantics, vmem_bytes):
    limit = int(min(max(vmem_bytes, 16 << 20), V7X_SCOPED_VMEM_BYTES))
    return pltpu.CompilerParams(dimension_semantics=semantics, vmem_limit_bytes=limit)


def _nbytes(shape, dtype):
    return math.prod(shape) * jnp.dtype(dtype).itemsize


def _ada_kernel(c_ref, w_ref, b_ref, o_ref):
    a = jax.nn.silu(c_ref[...]).astype(BF16)
    o_ref[...] = jnp.dot(a, w_ref[...].astype(BF16), preferred_element_type=F32) + b_ref[...]


def _ada_call(c, w, b, tn=512):
    rows, d = c.shape
    n = w.shape[1]
    vmem = 2 * (_nbytes((rows, d), F32) + _nbytes((d, tn), F32) + _nbytes((rows, tn), F32)) \
        + _nbytes((d, tn), BF16) + _nbytes((rows, d), BF16)
    return pl.pallas_call(
        _ada_kernel,
        out_shape=jax.ShapeDtypeStruct((rows, n), F32),
        grid=(n // tn,),
        in_specs=[pl.BlockSpec((rows, d), lambda j: (0, 0)),
                  pl.BlockSpec((d, tn), lambda j: (0, j)),
                  pl.BlockSpec((1, tn), lambda j: (0, j))],
        out_specs=pl.BlockSpec((rows, tn), lambda j: (0, j)),
        compiler_params=_params(("arbitrary",), vmem),
        name="ada_proj",
    )(c, w, b.reshape(1, n))


def _modulate_kernel(x_ref, s_ref, b_ref, u_ref):
    u_ref[...] = (x_ref[...] * (1.0 + s_ref[...]) + b_ref[...]).astype(BF16)


def _ada_spec(la, tl, d):
    if la == 1:
        return pl.BlockSpec((1, 1, d), lambda b, t: (b, 0, 0))
    return pl.BlockSpec((1, tl, d), lambda b, t: (b, t, 0))


def _modulate_call(x, scale, shift, tl):
    bsz, l, d = x.shape
    la = scale.shape[1]
    xs = pl.BlockSpec((1, tl, d), lambda b, t: (b, t, 0))
    vmem = 2 * (_nbytes((tl, d), F32) + _nbytes((tl, d), BF16) + 2 * _nbytes((min(la, tl), d), F32)) \
        + _nbytes((tl, d), F32)
    return pl.pallas_call(
        _modulate_kernel,
        out_shape=jax.ShapeDtypeStruct((bsz, l, d), BF16),
        grid=(bsz, l // tl),
        in_specs=[xs, _ada_spec(la, tl, d), _ada_spec(la, tl, d)],
        out_specs=xs,
        compiler_params=_params(("parallel", "parallel"), vmem),
        name="modulate",
    )(x, scale, shift)


def _ln_kernel(alpha, coef, has_next, x_ref, y_ref, g_ref, lng_ref, lnb_ref, *rest):
    if has_next:
        s_ref, b_ref, xo_ref, uo_ref = rest
    else:
        (xo_ref,) = rest
    z = alpha * x_ref[...] + (coef * (1.0 + g_ref[...])) * y_ref[...]
    mu = jnp.mean(z, axis=-1, keepdims=True)
    zc = z - mu
    var = jnp.mean(zc * zc, axis=-1, keepdims=True)
    xn = zc * lax.rsqrt(var + LN_EPS) * lng_ref[...] + lnb_ref[...]
    xo_ref[...] = xn
    if has_next:
        uo_ref[...] = (xn * (1.0 + s_ref[...]) + b_ref[...]).astype(BF16)


def _ln_call(x, y, gate, ln_g, ln_b, alpha, coef, tl, nxt=None):
    bsz, l, d = x.shape
    la = gate.shape[1]
    xs = pl.BlockSpec((1, tl, d), lambda b, t: (b, t, 0))
    vec = pl.BlockSpec((1, 1, d), lambda b, t: (0, 0, 0))
    ins = [x, y, gate, ln_g.reshape(1, 1, d), ln_b.reshape(1, 1, d)]
    in_specs = [xs, xs, _ada_spec(la, tl, d), vec, vec]
    out_shape = [jax.ShapeDtypeStruct((bsz, l, d), F32)]
    out_specs = [xs]
    if nxt is not None:
        ins += list(nxt)
        in_specs += [_ada_spec(la, tl, d), _ada_spec(la, tl, d)]
        out_shape.append(jax.ShapeDtypeStruct((bsz, l, d), BF16))
        out_specs.append(xs)
    vmem = 2 * (3 * _nbytes((tl, d), F32) + _nbytes((tl, d), BF16)
                + 3 * _nbytes((min(la, tl), d), F32)) + 3 * _nbytes((tl, d), F32)
    outs = pl.pallas_call(
        functools.partial(_ln_kernel, alpha, coef, nxt is not None),
        out_shape=out_shape,
        grid=(bsz, l // tl),
        in_specs=in_specs,
        out_specs=out_specs,
        compiler_params=_params(("parallel", "parallel"), vmem),
        name="ln_modulate",
    )(*ins)
    return outs if nxt is not None else (outs[0], None)


def _wstat_kernel(n_a, n_w, epilogue, *refs):
    a_refs = refs[:n_a]
    w_refs = refs[n_a:n_a + n_w]
    o_ref = refs[n_a + n_w]
    w_bf = refs[n_a + n_w + 1:]

    @pl.when(pl.program_id(1) == 0)
    def _():
        for w_ref, s_ref in zip(w_refs, w_bf):
            s_ref[...] = w_ref[...].astype(BF16)

    o_ref[...] = epilogue(a_refs, w_bf).astype(o_ref.dtype)


def _swiglu_epilogue(a_refs, w_bf):
    u = a_refs[0][...]
    g = jnp.dot(u, w_bf[0][...], preferred_element_type=F32)
    v = jnp.dot(u, w_bf[1][...], preferred_element_type=F32)
    return jax.nn.silu(g) * v


def _sum_epilogue(a_refs, w_bf):
    acc = None
    for a_ref, w_ref in zip(a_refs, w_bf):
        d = jnp.dot(a_ref[...], w_ref[...], preferred_element_type=F32)
        acc = d if acc is None else acc + d
    return acc


def _wstat_call(a_list, w_list, w_row_blocks, col_off, n_out, epilogue, out_dtype, tm, tn, name):
    m = a_list[0].shape[0]
    ks = [a.shape[1] for a in a_list]
    w_ks = [ks[i % len(ks)] for i in range(len(w_list))]
    off = col_off // tn
    in_specs = [pl.BlockSpec((tm, k), lambda n, i: (i, 0)) for k in ks]
    in_specs += [pl.BlockSpec((k, tn), functools.partial(lambda rb, n, i: (rb, off + n), rb))
                 for k, rb in zip(w_ks, w_row_blocks)]
    vmem = sum(2 * _nbytes((tm, k), BF16) for k in ks) \
        + sum(2 * _nbytes((k, tn), F32) + 2 * _nbytes((k, tn), BF16) for k in w_ks) \
        + 2 * _nbytes((tm, tn), out_dtype) + 4 * _nbytes((tm, tn), F32)
    return pl.pallas_call(
        functools.partial(_wstat_kernel, len(a_list), len(w_list), epilogue),
        out_shape=jax.ShapeDtypeStruct((m, n_out), out_dtype),
        grid=(n_out // tn, m // tm),
        in_specs=in_specs,
        out_specs=pl.BlockSpec((tm, tn), lambda n, i: (i, n)),
        scratch_shapes=[pltpu.VMEM((k, tn), BF16) for k in w_ks],
        compiler_params=_params(("arbitrary", "arbitrary"), vmem),
        name=name,
    )(*a_list, *w_list)


def _rstat_kernel(h_ref, w_ref, y_ref):
    y_ref[...] = jnp.dot(h_ref[...], w_ref[...].astype(BF16), preferred_element_type=F32)


def _rstat_call(h, w, tm, tn, name):
    m, k = h.shape
    n = w.shape[1]
    single = m // tm > 1
    a_spec = pl.BlockSpec((tm, k), lambda i, j: (i, 0),
                          pipeline_mode=pl.Buffered(1) if single else None)
    vmem = (1 if single else 2) * _nbytes((tm, k), BF16) + 2 * _nbytes((k, tn), F32) \
        + _nbytes((k, tn), BF16) + 3 * _nbytes((tm, tn), F32)
    return pl.pallas_call(
        _rstat_kernel,
        out_shape=jax.ShapeDtypeStruct((m, n), F32),
        grid=(m // tm, n // tn),
        in_specs=[a_spec, pl.BlockSpec((k, tn), lambda i, j: (0, j))],
        out_specs=pl.BlockSpec((tm, tn), lambda i, j: (i, j)),
        compiler_params=_params(("arbitrary", "arbitrary"), vmem),
        name=name,
    )(h, w)


def _pattern_bias_tiles():
    tb = ATTN_BLOCK
    i = np.arange(tb)[:, None]
    j = np.arange(tb)[None, :]
    tiles = []
    for d in range(4):
        delta = d * tb + i - j
        mult = np.zeros((tb, tb), np.float64)
        for win, dil in DILATED_PATTERNS:
            mult += (delta >= 0) & (delta % dil == 0) & (delta <= win)
        with np.errstate(divide="ignore"):
            tiles.append(np.log(mult))
    assert 3 * tb - (tb - 1) > DILATED_PATTERNS[1][0]
    return np.stack(tiles).astype(np.float32)


def _prompt_attn_kernel(seq, scale, q_ref, k_ref, v_ref, bias_ref, o_ref):
    tb = ATTN_BLOCK
    kb = k_ref[...].astype(BF16)
    vb = v_ref[...].astype(BF16)
    for i in range(seq // tb):
        n = (i + 1) * tb
        qb = q_ref[i * tb:(i + 1) * tb, :].astype(BF16)
        s = lax.dot_general(qb, kb[:n], (((1,), (1,)), ((), ())), preferred_element_type=F32)
        bias = jnp.concatenate([bias_ref[min(i - j, 3)] for j in range(i + 1)], axis=1)
        s = s * scale + bias
        m = jnp.max(s, axis=-1, keepdims=True)
        p = jnp.exp(s - m)
        l = jnp.sum(p, axis=-1, keepdims=True)
        acc = jnp.dot(p.astype(BF16), vb[:n], preferred_element_type=F32)
        o_ref[i * tb:(i + 1) * tb, :] = (acc / l).astype(o_ref.dtype)


def _prompt_attn_call(q, k, v, n_heads, head_dim):
    bsz, seq, width = q.shape
    assert width == n_heads * head_dim and head_dim == LANE and seq % ATTN_BLOCK == 0
    assert seq <= DILATED_PATTERNS[2][0]
    bias = jnp.asarray(_pattern_bias_tiles())
    hs = pl.BlockSpec((None, seq, head_dim), lambda b, h: (b, 0, h))
    vmem = 2 * (3 * _nbytes((seq, head_dim), F32) + _nbytes((seq, head_dim), BF16)
                + _nbytes(bias.shape, F32)) + 2 * _nbytes((seq, head_dim), BF16) \
        + 4 * _nbytes((ATTN_BLOCK, seq), F32)
    return pl.pallas_call(
        functools.partial(_prompt_attn_kernel, seq, head_dim ** -0.5),
        out_shape=jax.ShapeDtypeStruct((bsz, seq, width), BF16),
        grid=(bsz, n_heads),
        in_specs=[hs, hs, hs, pl.BlockSpec(bias.shape, lambda b, h: (0, 0, 0))],
        out_specs=hs,
        compiler_params=_params(("parallel", "parallel"), vmem),
        name="prompt_attention",
    )(q, k, v, bias)


def _prompt_pool_kernel(ts, gdim, x_ref, halo_ref, w_ref, sc_ref, o_ref):
    t = pl.program_id(1)
    halo = jnp.where(t > 0, halo_ref[...], 0.0)
    row = t * ts + lax.broadcasted_iota(jnp.int32, (ts, 1), 0)
    for g, win in enumerate(POOL_WINDOWS):
        lanes = slice(g * gdim, (g + 1) * gdim)
        x = x_ref[:, lanes]
        run = jnp.concatenate([halo[:, lanes], x], axis=0)
        span = 1
        while span < win:
            run = run[span:] + run[:-span]
            span *= 2
        total = run[POOL_MAX - (win - 1):POOL_MAX - (win - 1) + ts]
        cnt = jnp.minimum(row + 1, win).astype(F32)
        d = total / cnt - x
        y = jnp.dot(d.astype(BF16), w_ref[g].astype(BF16), preferred_element_type=F32)
        o_ref[:, lanes] = (y * sc_ref[:, lanes]).astype(o_ref.dtype)


def _prompt_pool_call(p, pool_w, pool_scale, ts=256):
    bsz, seq, width = p.shape
    groups, gdim, _ = pool_w.shape
    hb = ts // POOL_MAX
    vmem = 2 * (_nbytes((ts, width), F32) + _nbytes((POOL_MAX, width), F32) + _nbytes(pool_w.shape, F32)
                + _nbytes((ts, width), BF16)) + 8 * _nbytes((ts + POOL_MAX, gdim), F32)
    return pl.pallas_call(
        functools.partial(_prompt_pool_kernel, ts, gdim),
        out_shape=jax.ShapeDtypeStruct((bsz, seq, width), BF16),
        grid=(bsz, seq // ts),
        in_specs=[pl.BlockSpec((None, ts, width), lambda b, t: (b, t, 0)),
                  pl.BlockSpec((None, POOL_MAX, width), lambda b, t: (b, jnp.maximum(t * hb - 1, 0), 0)),
                  pl.BlockSpec(pool_w.shape, lambda b, t: (0, 0, 0)),
                  pl.BlockSpec((1, width), lambda b, t: (0, 0))],
        out_specs=pl.BlockSpec((None, ts, width), lambda b, t: (b, t, 0)),
        compiler_params=_params(("parallel", "arbitrary"), vmem),
        name="prompt_pool",
    )(p, p, pool_w, pool_scale.reshape(1, width))


def _sample_attn_kernel(n_heads, head_dim, scale, q_ref, kn_ref, vn_ref, *rest):
    k_refs, v_refs, o_ref = rest[0:3], rest[3:6], rest[6]
    width = n_heads * head_dim
    head_of_lane = lax.broadcasted_iota(jnp.int32, (n_heads, width), 1) // head_dim
    own = head_of_lane == lax.broadcasted_iota(jnp.int32, (n_heads, width), 0)
    q_bd = jnp.where(own, q_ref[...], 0.0).astype(BF16)
    kn = kn_ref[...].astype(BF16).astype(F32)
    vn = vn_ref[...].astype(BF16).astype(F32)
    s_new = jnp.sum(q_bd.astype(F32) * kn, axis=-1, keepdims=True) * scale
    outs, lses = [], []
    for k_ref, v_ref in zip(k_refs, v_refs):
        s = lax.dot_general(q_bd, k_ref[...].astype(BF16), (((1,), (1,)), ((), ())),
                            preferred_element_type=F32) * scale
        m = jnp.maximum(jnp.max(s, axis=-1, keepdims=True), s_new)
        e = jnp.exp(s - m)
        e_new = jnp.exp(s_new - m)
        l = jnp.sum(e, axis=-1, keepdims=True) + e_new
        o = jnp.dot((e / l).astype(BF16), v_ref[...].astype(BF16), preferred_element_type=F32)
        o = o + (e_new / l).astype(BF16).astype(F32) * vn
        outs.append(o)
        lses.append(m + jnp.log(l))
    top = functools.reduce(jnp.maximum, lses)
    ws = [jnp.exp(x - top) for x in lses]
    den = functools.reduce(lambda a, b: a + b, ws)
    mix = functools.reduce(lambda a, b: a + b, [(w / den) * o for w, o in zip(ws, outs)])
    o_ref[...] = jnp.sum(jnp.where(own, mix, 0.0), axis=0, keepdims=True).astype(o_ref.dtype)


def _sample_attn_call(q, k_new, v_new, cache_k, cache_v, n_heads, head_dim):
    bsz, wb, width = cache_k.shape
    rows = DILATED_PATTERNS[0][0]
    assert wb == WIN_MAX and all(win // dil == rows for win, dil in DILATED_PATTERNS)
    row_spec = pl.BlockSpec((None, 1, width), lambda b: (b, 0, 0))
    views, specs = [], []
    for cache in (cache_k, cache_v):
        for win, dil in DILATED_PATTERNS:
            views.append(cache.reshape(bsz, wb // dil, dil * width))
            blk = wb // dil // rows - 1
            specs.append(pl.BlockSpec((None, rows, width), functools.partial(lambda blk, b: (b, blk, 0), blk)))
    vmem = 2 * 6 * _nbytes((rows, width), F32) + 6 * _nbytes((rows, width), BF16) \
        + 16 * _nbytes((n_heads, width), F32) + 8 * _nbytes((1, width), F32)
    out = pl.pallas_call(
        functools.partial(_sample_attn_kernel, n_heads, head_dim, head_dim ** -0.5),
        out_shape=jax.ShapeDtypeStruct((bsz, 1, width), BF16),
        grid=(bsz,),
        in_specs=[row_spec, row_spec, row_spec] + specs,
        out_specs=row_spec,
        compiler_params=_params(("parallel",), vmem),
        name="sample_attention",
    )(q.reshape(bsz, 1, width), k_new.reshape(bsz, 1, width), v_new.reshape(bsz, 1, width), *views)
    return out.reshape(bsz, width)


def _sample_pool_kernel(gdim, st_ref, x_ref, w_ref, sc_ref, o_ref):
    nbuf = st_ref.shape[0]
    for g, win in enumerate(POOL_WINDOWS):
        lanes = slice(g * gdim, (g + 1) * gdim)
        x = x_ref[:, lanes]
        total = x
        for back in range(1, win):
            total = total + st_ref[nbuf - back, :, lanes]
        d = total / float(win) - x
        y = jnp.dot(d.astype(BF16), w_ref[g].astype(BF16), preferred_element_type=F32)
        o_ref[:, lanes] = (y * sc_ref[:, lanes]).astype(o_ref.dtype)


def _sample_pool_call(state, x, pool_w, pool_scale):
    nbuf, bsz, width = state.shape
    assert nbuf == POOL_MAX - 1
    groups, gdim, _ = pool_w.shape
    full = lambda shape: pl.BlockSpec(shape, lambda i: (0,) * len(shape))
    vmem = 2 * (_nbytes((POOL_MAX, bsz, width), F32) + 2 * _nbytes((bsz, width), F32)
                + _nbytes(pool_w.shape, F32)) + _nbytes(pool_w.shape, BF16)
    return pl.pallas_call(
        functools.partial(_sample_pool_kernel, gdim),
        out_shape=jax.ShapeDtypeStruct((bsz, width), BF16),
        grid=(1,),
        in_specs=[full(state.shape), full(x.shape), full(pool_w.shape), full((1, width))],
        out_specs=full((bsz, width)),
        compiler_params=_params(("arbitrary",), vmem),
        name="sample_pool",
    )(state, x, pool_w, pool_scale.reshape(1, width))


def _swiglu(u2d, w_gate, w_up, w_down, tm_up, tm_down, tag):
    d_ff = w_gate.shape[1]
    h = _wstat_call([u2d], [w_gate, w_up], [0, 0], 0, d_ff, _swiglu_epilogue, BF16,
                    tm_up, MXU_DIM, "swiglu_gate_up_" + tag)
    return _rstat_call(h, w_down, tm_down, MXU_DIM, "swiglu_down_" + tag)


def _project(u2d, w_in, col_off, width, tm, tag):
    return _wstat_call([u2d], [w_in], [0], col_off, width, _sum_epilogue, F32, tm, 512, "mixer_in_" + tag)


def _group_layer(x, ada, weights, mixers, tl, tm, tm_down, tag):
    (w_g1, w_u1, w_d1, ln1_g, ln1_b, w_in, w_out, ln2_g, ln2_b, w_g2, w_u2, w_d2, ln3_g, ln3_b,
     alpha, attn_width) = weights
    bsz, l, d = x.shape
    m = bsz * l
    u = _modulate_call(x, ada[1], ada[0], tl)
    y = _swiglu(u.reshape(m, d), w_g1, w_u1, w_d1, tm, tm_down, tag)
    x, u = _ln_call(x, y.reshape(bsz, l, d), ada[2], ln1_g, ln1_b, alpha, 0.5, tl, nxt=(ada[4], ada[3]))
    u2d = u.reshape(m, d)
    q, k, v = (_project(u2d, w_in, i * attn_width, attn_width, tm, tag) for i in range(3))
    p = _project(u2d, w_in, 3 * attn_width, d - attn_width, tm, tag)
    attn, pool = mixers(q, k, v, p)
    y = _wstat_call([attn, pool], [w_out, w_out], [0, 1], 0, d, _sum_epilogue, F32, tm, 512, "mixer_out_" + tag)
    x, u = _ln_call(x, y.reshape(bsz, l, d), ada[5], ln2_g, ln2_b, alpha, 1.0, tl, nxt=(ada[7], ada[6]))
    y = _swiglu(u.reshape(m, d), w_g2, w_u2, w_d2, tm, tm_down, tag)
    x, _ = _ln_call(x, y.reshape(bsz, l, d), ada[8], ln3_g, ln3_b, alpha, 0.5, tl)
    return x, k, v, p


def kernel(x_prompt, x_sample, cache_k, cache_v, state_pool, c_prompt, c_sample,
           w_ada, b_ada, ffn1_w_gate, ffn1_w_up, ffn1_w_down, ln1_g, ln1_b,
           w_in, pool_w, pool_scale, w_out, ln2_g, ln2_b,
           ffn2_w_gate, ffn2_w_up, ffn2_w_down, ln3_g, ln3_b):
    depth = w_ada.shape[0]
    bp, seq, d = x_prompt.shape
    bs, t_new, _ = x_sample.shape
    n_heads, head_dim = cache_k.shape[3], cache_k.shape[4]
    attn_width = n_heads * head_dim
    pool_width = d - attn_width
    assert t_new == 1 and seq == WIN_MAX
    alpha = (2 * depth) ** 0.25
    keep_pool = min(POOL_MAX - 1, seq)

    xp = x_prompt
    xs = x_sample.reshape(1, bs, d)
    rows = bp + bs
    rows_pad = -(-rows // 16) * 16
    c_all = jnp.pad(jnp.concatenate([c_prompt, c_sample], axis=0), ((0, rows_pad - rows), (0, 0)))

    outs = [[] for _ in range(6)]
    for l in range(depth):
        ada = _ada_call(c_all, w_ada[l], b_ada[l])
        ada_p = [ada[:bp, i * d:(i + 1) * d].reshape(bp, 1, d) for i in range(N_ADA)]
        ada_s = [ada[bp:rows, i * d:(i + 1) * d].reshape(1, bs, d) for i in range(N_ADA)]
        weights = (ffn1_w_gate[l], ffn1_w_up[l], ffn1_w_down[l], ln1_g[l], ln1_b[l], w_in[l], w_out[l],
                   ln2_g[l], ln2_b[l], ffn2_w_gate[l], ffn2_w_up[l], ffn2_w_down[l], ln3_g[l], ln3_b[l],
                   alpha, attn_width)

        def prompt_mixers(q, k, v, p):
            attn = _prompt_attn_call(q.reshape(bp, seq, attn_width), k.reshape(bp, seq, attn_width),
                                     v.reshape(bp, seq, attn_width), n_heads, head_dim)
            pool = _prompt_pool_call(p.reshape(bp, seq, pool_width), pool_w[l], pool_scale[l])
            return attn.reshape(bp * seq, attn_width), pool.reshape(bp * seq, pool_width)

        def sample_mixers(q, k, v, p):
            attn = _sample_attn_call(q, k, v, cache_k[l].reshape(bs, -1, attn_width),
                                     cache_v[l].reshape(bs, -1, attn_width), n_heads, head_dim)
            pool = _sample_pool_call(jnp.swapaxes(state_pool[l], 0, 1), p, pool_w[l], pool_scale[l])
            return attn, pool

        xp, kp, vp, pp = _group_layer(xp, ada_p, weights, prompt_mixers, 256, 1024, 1024, "prompt")
        xs, ks, vs, ps = _group_layer(xs, ada_s, weights, sample_mixers, bs, bs, bs, "sample")
        outs[0].append(kp.reshape(bp, seq, n_heads, head_dim))
        outs[1].append(vp.reshape(bp, seq, n_heads, head_dim))
        outs[2].append(pp.reshape(bp, seq, pool_width)[:, seq - keep_pool:])
        outs[3].append(ks.reshape(bs, 1, n_heads, head_dim))
        outs[4].append(vs.reshape(bs, 1, n_heads, head_dim))
        outs[5].append(ps.reshape(bs, 1, pool_width))
    return (xp, xs.reshape(bs, 1, d)) + tuple(jnp.stack(o) for o in outs)
```

```python
import functools
import math

import numpy as np
import jax
import jax.numpy as jnp
from jax import lax
from jax.experimental import pallas as pl
from jax.experimental.pallas import tpu as pltpu

F32 = jnp.float32
BF16 = jnp.bfloat16

V7X_SCOPED_VMEM_BYTES = 60000 * 1024
LANE = 128
MXU_DIM = 256

LN_EPS = 1e-5
N_ADA = 9
POOL_WINDOWS = (2, 4, 8, 16)
POOL_MAX = 16
DILATED_PATTERNS = ((128, 1), (512, 4), (2048, 16))
WIN_MAX = 2048
ATTN_BLOCK = 256


def _params(semantics, vmem_bytes):
    limit = int(min(max(vmem_bytes, 16 << 20), V7X_SCOPED_VMEM_BYTES))
    return pltpu.CompilerParams(dimension_semantics=semantics, vmem_limit_bytes=limit)


def _nbytes(shape, dtype):
    return math.prod(shape) * jnp.dtype(dtype).itemsize


def _ada_kernel(c_ref, w_ref, b_ref, o_ref):
    a = jax.nn.silu(c_ref[...]).astype(BF16)
    o_ref[...] = jnp.dot(a, w_ref[...].astype(BF16), preferred_element_type=F32) + b_ref[...]


def _ada_call(c, w, b, tn=512):
    rows, d = c.shape
    n = w.shape[1]
    vmem = 2 * (_nbytes((rows, d), F32) + _nbytes((d, tn), F32) + _nbytes((rows, tn), F32)) \
        + _nbytes((d, tn), BF16) + _nbytes((rows, d), BF16)
    return pl.pallas_call(
        _ada_kernel,
        out_shape=jax.ShapeDtypeStruct((rows, n), F32),
        grid=(n // tn,),
        in_specs=[pl.BlockSpec((rows, d), lambda j: (0, 0)),
                  pl.BlockSpec((d, tn), lambda j: (0, j)),
                  pl.BlockSpec((1, tn), lambda j: (0, j))],
        out_specs=pl.BlockSpec((rows, tn), lambda j: (0, j)),
        compiler_params=_params(("arbitrary",), vmem),
        name="ada_proj",
    )(c, w, b.reshape(1, n))


def _modulate_kernel(x_ref, s_ref, b_ref, u_ref):
    u_ref[...] = (x_ref[...] * (1.0 + s_ref[...]) + b_ref[...]).astype(BF16)


def _ada_spec(la, tl, d):
    if la == 1:
        return pl.BlockSpec((1, 1, d), lambda b, t: (b, 0, 0))
    return pl.BlockSpec((1, tl, d), lambda b, t: (b, t, 0))


def _modulate_call(x, scale, shift, tl):
    bsz, l, d = x.shape
    la = scale.shape[1]
    xs = pl.BlockSpec((1, tl, d), lambda b, t: (b, t, 0))
    vmem = 2 * (_nbytes((tl, d), F32) + _nbytes((tl, d), BF16) + 2 * _nbytes((min(la, tl), d), F32)) \
        + _nbytes((tl, d), F32)
    return pl.pallas_call(
        _modulate_kernel,
        out_shape=jax.ShapeDtypeStruct((bsz, l, d), BF16),
        grid=(bsz, l // tl),
        in_specs=[xs, _ada_spec(la, tl, d), _ada_spec(la, tl, d)],
        out_specs=xs,
        compiler_params=_params(("parallel", "parallel"), vmem),
        name="modulate",
    )(x, scale, shift)


def _ln_kernel(alpha, coef, has_next, x_ref, y_ref, g_ref, lng_ref, lnb_ref, *rest):
    if has_next:
        s_ref, b_ref, xo_ref, uo_ref = rest
    else:
        (xo_ref,) = rest
    z = alpha * x_ref[...] + (coef * (1.0 + g_ref[...])) * y_ref[...]
    mu = jnp.mean(z, axis=-1, keepdims=True)
    zc = z - mu
    var = jnp.mean(zc * zc, axis=-1, keepdims=True)
    xn = zc * lax.rsqrt(var + LN_EPS) * lng_ref[...] + lnb_ref[...]
    xo_ref[...] = xn
    if has_next:
        uo_ref[...] = (xn * (1.0 + s_ref[...]) + b_ref[...]).astype(BF16)


def _ln_call(x, y, gate, ln_g, ln_b, alpha, coef, tl, nxt=None):
    bsz, l, d = x.shape
    la = gate.shape[1]
    xs = pl.BlockSpec((1, tl, d), lambda b, t: (b, t, 0))
    vec = pl.BlockSpec((1, 1, d), lambda b, t: (0, 0, 0))
    ins = [x, y, gate, ln_g.reshape(1, 1, d), ln_b.reshape(1, 1, d)]
    in_specs = [xs, xs, _ada_spec(la, tl, d), vec, vec]
    out_shape = [jax.ShapeDtypeStruct((bsz, l, d), F32)]
    out_specs = [xs]
    if nxt is not None:
        ins += list(nxt)
        in_specs += [_ada_spec(la, tl, d), _ada_spec(la, tl, d)]
        out_shape.append(jax.ShapeDtypeStruct((bsz, l, d), BF16))
        out_specs.append(xs)
    vmem = 2 * (3 * _nbytes((tl, d), F32) + _nbytes((tl, d), BF16)
                + 3 * _nbytes((min(la, tl), d), F32)) + 3 * _nbytes((tl, d), F32)
    outs = pl.pallas_call(
        functools.partial(_ln_kernel, alpha, coef, nxt is not None),
        out_shape=out_shape,
        grid=(bsz, l // tl),
        in_specs=in_specs,
        out_specs=out_specs,
        compiler_params=_params(("parallel", "parallel"), vmem),
        name="ln_modulate",
    )(*ins)
    return outs if nxt is not None else (outs[0], None)


def _swiglu_epilogue(a_vals, w_vals):
    g = jnp.dot(a_vals[0], w_vals[0], preferred_element_type=F32)
    v = jnp.dot(a_vals[0], w_vals[1], preferred_element_type=F32)
    return jax.nn.silu(g) * v


def _sum_epilogue(a_vals, w_vals):
    acc = None
    for a, w in zip(a_vals, w_vals):
        d = jnp.dot(a, w, preferred_element_type=F32)
        acc = d if acc is None else acc + d
    return acc


def _wstat_kernel(n_a, n_w, epilogue, *refs):
    a_refs = refs[:n_a]
    as_refs = refs[n_a:2 * n_a]
    w_refs = refs[2 * n_a:2 * n_a + n_w]
    o_ref, os_ref = refs[2 * n_a + n_w:2 * n_a + n_w + 2]
    w_bf = refs[2 * n_a + n_w + 2:]

    @pl.when(pl.program_id(1) == 0)
    def _():
        for w_ref, s_ref in zip(w_refs, w_bf):
            s_ref[...] = w_ref[...].astype(BF16)
        os_ref[...] = epilogue([r[...] for r in as_refs], [s[...] for s in w_bf]).astype(os_ref.dtype)

    o_ref[...] = epilogue([r[...] for r in a_refs], [s[...] for s in w_bf]).astype(o_ref.dtype)


def _w_specs(w_list, w_ks, w_row_blocks, off, tn, n_axis):
    def index(rb, *ids):
        return (rb, off + ids[n_axis])
    return [pl.BlockSpec((k, tn), functools.partial(index, rb)) for k, rb in zip(w_ks, w_row_blocks)]


def _wstat_call(a_list, as_list, w_list, w_row_blocks, col_off, n_out, epilogue, out_dtype, tm, tn, name):
    m = a_list[0].shape[0]
    ms = as_list[0].shape[0]
    ks = [a.shape[1] for a in a_list]
    w_ks = [ks[i % len(ks)] for i in range(len(w_list))]
    in_specs = [pl.BlockSpec((tm, k), lambda n, i: (i, 0)) for k in ks]
    in_specs += [pl.BlockSpec((ms, k), lambda n, i: (0, 0)) for k in ks]
    in_specs += _w_specs(w_list, w_ks, w_row_blocks, col_off // tn, tn, 0)
    vmem = sum(2 * _nbytes((tm + ms, k), BF16) for k in ks) \
        + sum(2 * _nbytes((k, tn), F32) + 2 * _nbytes((k, tn), BF16) for k in w_ks) \
        + 2 * _nbytes((tm + ms, tn), out_dtype) + 4 * _nbytes((tm, tn), F32)
    return pl.pallas_call(
        functools.partial(_wstat_kernel, len(a_list), len(w_list), epilogue),
        out_shape=[jax.ShapeDtypeStruct((m, n_out), out_dtype),
                   jax.ShapeDtypeStruct((n_out // tn, ms, tn), out_dtype)],
        grid=(n_out // tn, m // tm),
        in_specs=in_specs,
        out_specs=[pl.BlockSpec((tm, tn), lambda n, i: (i, n)),
                   pl.BlockSpec((None, ms, tn), lambda n, i: (n, 0, 0))],
        scratch_shapes=[pltpu.VMEM((k, tn), BF16) for k in w_ks],
        compiler_params=_params(("arbitrary", "arbitrary"), vmem),
        name=name,
    )(*a_list, *as_list, *w_list)


def _rstat_kernel(n_a, n_w, epilogue, sub, *refs):
    a_refs = refs[:n_a]
    as_refs = refs[n_a:2 * n_a]
    w_refs = refs[2 * n_a:2 * n_a + n_w]
    o_ref, os_ref = refs[2 * n_a + n_w:]
    rows = o_ref.shape[0] // sub
    w_vals = [w_ref[...].astype(BF16) for w_ref in w_refs]
    for r in range(sub):
        a_vals = [a_ref[r * rows:(r + 1) * rows, :] for a_ref in a_refs]
        o_ref[r * rows:(r + 1) * rows, :] = epilogue(a_vals, w_vals).astype(o_ref.dtype)

    @pl.when(pl.program_id(0) == 0)
    def _():
        ws = [w_ref[...].astype(BF16) for w_ref in w_refs]
        os_ref[pl.program_id(1)] = epilogue([r[...] for r in as_refs], ws).astype(os_ref.dtype)


def _rstat_call(a_list, as_list, w_list, w_row_blocks, col_off, n_out, epilogue, out_dtype, tm, tn, sub, name):
    m = a_list[0].shape[0]
    ms = as_list[0].shape[0]
    ks = [a.shape[1] for a in a_list]
    w_ks = [ks[i % len(ks)] for i in range(len(w_list))]
    in_specs = [pl.BlockSpec((tm, k), lambda i, n: (i, 0), pipeline_mode=pl.Buffered(1)) for k in ks]
    in_specs += [pl.BlockSpec((ms, k), lambda i, n: (0, 0)) for k in ks]
    in_specs += _w_specs(w_list, w_ks, w_row_blocks, col_off // tn, tn, 1)
    vmem = sum(_nbytes((tm, k), BF16) + 2 * _nbytes((ms, k), BF16) for k in ks) \
        + sum(2 * _nbytes((k, tn), F32) + _nbytes((k, tn), BF16) for k in w_ks) \
        + 2 * _nbytes((tm, tn), out_dtype) + 2 * _nbytes((ms, n_out), out_dtype) \
        + 6 * _nbytes((tm // sub, tn), F32)
    return pl.pallas_call(
        functools.partial(_rstat_kernel, len(a_list), len(w_list), epilogue, sub),
        out_shape=[jax.ShapeDtypeStruct((m, n_out), out_dtype),
                   jax.ShapeDtypeStruct((n_out // tn, ms, tn), out_dtype)],
        grid=(m // tm, n_out // tn),
        in_specs=in_specs,
        out_specs=[pl.BlockSpec((tm, tn), lambda i, n: (i, n)),
                   pl.BlockSpec((n_out // tn, ms, tn), lambda i, n: (0, 0, 0))],
        compiler_params=_params(("arbitrary", "arbitrary"), vmem),
        name=name,
    )(*a_list, *as_list, *w_list)


def _rows_major(x):
    nb, rows, tn = x.shape
    return jnp.swapaxes(x, 0, 1).reshape(rows, nb * tn)


def _pattern_bias_tiles():
    tb = ATTN_BLOCK
    i = np.arange(tb)[:, None]
    j = np.arange(tb)[None, :]
    tiles = []
    for d in range(4):
        delta = d * tb + i - j
        mult = np.zeros((tb, tb), np.float64)
        for win, dil in DILATED_PATTERNS:
            mult += (delta >= 0) & (delta % dil == 0) & (delta <= win)
        with np.errstate(divide="ignore"):
            tiles.append(np.log(mult))
    assert 3 * tb - (tb - 1) > DILATED_PATTERNS[1][0]
    return np.stack(tiles).astype(np.float32)


def _prompt_attn_kernel(seq, scale, q_ref, k_ref, v_ref, bias_ref, o_ref):
    tb = ATTN_BLOCK
    kb = k_ref[...].astype(BF16)
    vb = v_ref[...].astype(BF16)
    for i in range(seq // tb):
        n = (i + 1) * tb
        qb = q_ref[i * tb:(i + 1) * tb, :].astype(BF16)
        s = lax.dot_general(qb, kb[:n], (((1,), (1,)), ((), ())), preferred_element_type=F32)
        bias = jnp.concatenate([bias_ref[min(i - j, 3)] for j in range(i + 1)], axis=1)
        s = s * scale + bias
        m = jnp.max(s, axis=-1, keepdims=True)
        p = jnp.exp(s - m)
        l = jnp.sum(p, axis=-1, keepdims=True)
        acc = jnp.dot(p.astype(BF16), vb[:n], preferred_element_type=F32)
        o_ref[i * tb:(i + 1) * tb, :] = (acc / l).astype(o_ref.dtype)


def _prompt_attn_call(q, k, v, n_heads, head_dim):
    bsz, seq, width = q.shape
    assert width == n_heads * head_dim and head_dim == LANE and seq % ATTN_BLOCK == 0
    assert seq <= DILATED_PATTERNS[2][0]
    bias = jnp.asarray(_pattern_bias_tiles())
    hs = pl.BlockSpec((None, seq, head_dim), lambda b, h: (b, 0, h))
    vmem = 2 * (3 * _nbytes((seq, head_dim), F32) + _nbytes((seq, head_dim), BF16)
                + _nbytes(bias.shape, F32)) + 2 * _nbytes((seq, head_dim), BF16) \
        + 4 * _nbytes((ATTN_BLOCK, seq), F32)
    return pl.pallas_call(
        functools.partial(_prompt_attn_kernel, seq, head_dim ** -0.5),
        out_shape=jax.ShapeDtypeStruct((bsz, seq, width), BF16),
        grid=(bsz, n_heads),
        in_specs=[hs, hs, hs, pl.BlockSpec(bias.shape, lambda b, h: (0, 0, 0))],
        out_specs=hs,
        compiler_params=_params(("parallel", "parallel"), vmem),
        name="prompt_attention",
    )(q, k, v, bias)


def _prompt_pool_kernel(ts, gdim, x_ref, halo_ref, w_ref, sc_ref, o_ref):
    t = pl.program_id(1)
    halo = jnp.where(t > 0, halo_ref[...], 0.0)
    row = t * ts + lax.broadcasted_iota(jnp.int32, (ts, 1), 0)
    for g, win in enumerate(POOL_WINDOWS):
        lanes = slice(g * gdim, (g + 1) * gdim)
        x = x_ref[:, lanes]
        run = jnp.concatenate([halo[:, lanes], x], axis=0)
        span = 1
        while span < win:
            run = run[span:] + run[:-span]
            span *= 2
        total = run[POOL_MAX - (win - 1):POOL_MAX - (win - 1) + ts]
        cnt = jnp.minimum(row + 1, win).astype(F32)
        d = total / cnt - x
        y = jnp.dot(d.astype(BF16), w_ref[g].astype(BF16), preferred_element_type=F32)
        o_ref[:, lanes] = (y * sc_ref[:, lanes]).astype(o_ref.dtype)


def _prompt_pool_call(p, pool_w, pool_scale, ts=256):
    bsz, seq, width = p.shape
    groups, gdim, _ = pool_w.shape
    hb = ts // POOL_MAX
    vmem = 2 * (_nbytes((ts, width), F32) + _nbytes((POOL_MAX, width), F32) + _nbytes(pool_w.shape, F32)
                + _nbytes((ts, width), BF16)) + 8 * _nbytes((ts + POOL_MAX, gdim), F32)
    return pl.pallas_call(
        functools.partial(_prompt_pool_kernel, ts, gdim),
        out_shape=jax.ShapeDtypeStruct((bsz, seq, width), BF16),
        grid=(bsz, seq // ts),
        in_specs=[pl.BlockSpec((None, ts, width), lambda b, t: (b, t, 0)),
                  pl.BlockSpec((None, POOL_MAX, width), lambda b, t: (b, jnp.maximum(t * hb - 1, 0), 0)),
                  pl.BlockSpec(pool_w.shape, lambda b, t: (0, 0, 0)),
                  pl.BlockSpec((1, width), lambda b, t: (0, 0))],
        out_specs=pl.BlockSpec((None, ts, width), lambda b, t: (b, t, 0)),
        compiler_params=_params(("parallel", "arbitrary"), vmem),
        name="prompt_pool",
    )(p, p, pool_w, pool_scale.reshape(1, width))


def _bf16_round(x):
    return x.astype(BF16).astype(F32)


def _sample_attn_kernel(scale, q_ref, kn_ref, vn_ref, *rest):
    k_refs, v_refs, o_ref = rest[0:3], rest[3:6], rest[6]
    q = _bf16_round(q_ref[...])[None]
    kn = _bf16_round(kn_ref[...])[None]
    vn = _bf16_round(vn_ref[...])[None]
    s_new = jnp.sum(q * kn, axis=-1, keepdims=True) * scale
    outs, lses = [], []
    for k_ref, v_ref in zip(k_refs, v_refs):
        s = jnp.sum(q * _bf16_round(k_ref[...]), axis=-1, keepdims=True) * scale
        m = jnp.maximum(jnp.max(s, axis=0, keepdims=True), s_new)
        e = jnp.exp(s - m)
        e_new = jnp.exp(s_new - m)
        l = jnp.sum(e, axis=0, keepdims=True) + e_new
        o = jnp.sum(_bf16_round(e / l) * _bf16_round(v_ref[...]), axis=0, keepdims=True)
        outs.append(o + _bf16_round(e_new / l) * vn)
        lses.append(m + jnp.log(l))
    top = functools.reduce(jnp.maximum, lses)
    ws = [jnp.exp(x - top) for x in lses]
    den = functools.reduce(lambda a, b: a + b, ws)
    mix = functools.reduce(lambda a, b: a + b, [(w / den) * o for w, o in zip(ws, outs)])
    o_ref[...] = mix[0].astype(o_ref.dtype)


def _sample_attn_call(q, k_new, v_new, cache_k, cache_v):
    bsz, wb, n_heads, head_dim = cache_k.shape
    rows = DILATED_PATTERNS[0][0]
    assert wb == WIN_MAX and all(win // dil == rows for win, dil in DILATED_PATTERNS)
    row_spec = pl.BlockSpec((None, n_heads, head_dim), lambda b: (b, 0, 0))
    views, specs = [], []
    for cache in (cache_k, cache_v):
        for win, dil in DILATED_PATTERNS:
            views.append(cache.reshape(bsz, wb // dil, dil, n_heads, head_dim))
            blk = wb // dil // rows - 1
            specs.append(pl.BlockSpec((None, rows, None, n_heads, head_dim),
                                      functools.partial(lambda blk, b: (b, blk, 0, 0, 0), blk)))
    vmem = 2 * 6 * _nbytes((rows, n_heads, head_dim), F32) + 6 * _nbytes((rows, n_heads, head_dim), F32) \
        + 12 * _nbytes((rows, n_heads, LANE), F32)
    return pl.pallas_call(
        functools.partial(_sample_attn_kernel, head_dim ** -0.5),
        out_shape=jax.ShapeDtypeStruct((bsz, n_heads, head_dim), BF16),
        grid=(bsz,),
        in_specs=[row_spec, row_spec, row_spec] + specs,
        out_specs=row_spec,
        compiler_params=_params(("parallel",), vmem),
        name="sample_attention",
    )(q, k_new, v_new, *views)


def _sample_pool_kernel(gdim, st_ref, x_ref, w_ref, sc_ref, o_ref):
    nbuf = st_ref.shape[0]
    for g, win in enumerate(POOL_WINDOWS):
        lanes = slice(g * gdim, (g + 1) * gdim)
        x = x_ref[:, lanes]
        total = x
        for back in range(1, win):
            total = total + st_ref[nbuf - back, :, lanes]
        d = total / float(win) - x
        y = jnp.dot(d.astype(BF16), w_ref[g].astype(BF16), preferred_element_type=F32)
        o_ref[:, lanes] = (y * sc_ref[:, lanes]).astype(o_ref.dtype)


def _sample_pool_call(state, x, pool_w, pool_scale):
    nbuf, bsz, width = state.shape
    assert nbuf == POOL_MAX - 1
    groups, gdim, _ = pool_w.shape
    full = lambda shape: pl.BlockSpec(shape, lambda i: (0,) * len(shape))
    vmem = 2 * (_nbytes((POOL_MAX, bsz, width), F32) + 2 * _nbytes((bsz, width), F32)
                + _nbytes(pool_w.shape, F32)) + _nbytes(pool_w.shape, BF16)
    return pl.pallas_call(
        functools.partial(_sample_pool_kernel, gdim),
        out_shape=jax.ShapeDtypeStruct((bsz, width), BF16),
        grid=(1,),
        in_specs=[full(state.shape), full(x.shape), full(pool_w.shape), full((1, width))],
        out_specs=full((bsz, width)),
        compiler_params=_params(("arbitrary",), vmem),
        name="sample_pool",
    )(state, x, pool_w, pool_scale.reshape(1, width))


PROMPT_LN_ROWS = 256
GATE_UP_ROWS = 2048
DOWN_ROWS = 1024
MIXER_ROWS = 1024
MIXER_COLS = 512


def _swiglu(u_p, u_s, w_gate, w_up, w_down):
    d_ff = w_gate.shape[1]
    d = w_down.shape[1]
    h_p, h_s = _rstat_call([u_p], [u_s], [w_gate, w_up], [0, 0], 0, d_ff, _swiglu_epilogue, BF16,
                           GATE_UP_ROWS, MXU_DIM, 4, "swiglu_gate_up")
    y_p, y_s = _rstat_call([h_p], [_rows_major(h_s)], [w_down], [0], 0, d, _sum_epilogue, F32,
                           DOWN_ROWS, MXU_DIM, 1, "swiglu_down")
    return y_p, _rows_major(y_s)


def _project(u_p, u_s, w_in, col_off, width):
    o_p, o_s = _wstat_call([u_p], [u_s], [w_in], [0], col_off, width, _sum_epilogue, F32,
                           MIXER_ROWS, MIXER_COLS, "mixer_in")
    return o_p, _rows_major(o_s)


def _layer(xp, xs, ada_p, ada_s, weights, prompt_mixers, sample_mixers):
    (w_g1, w_u1, w_d1, ln1_g, ln1_b, w_in, w_out, ln2_g, ln2_b, w_g2, w_u2, w_d2, ln3_g, ln3_b,
     alpha, attn_width) = weights
    bp, seq, d = xp.shape
    bs = xs.shape[1]
    mp = bp * seq
    tlp = PROMPT_LN_ROWS

    def ln_both(y_p, y_s, gate_i, ln_g, ln_b, coef, nxt_i):
        nonlocal xp, xs
        nxt_p = None if nxt_i is None else (ada_p[nxt_i + 1], ada_p[nxt_i])
        nxt_s = None if nxt_i is None else (ada_s[nxt_i + 1], ada_s[nxt_i])
        xp, u_p = _ln_call(xp, y_p.reshape(bp, seq, d), ada_p[gate_i], ln_g, ln_b, alpha, coef, tlp, nxt=nxt_p)
        xs, u_s = _ln_call(xs, y_s.reshape(1, bs, d), ada_s[gate_i], ln_g, ln_b, alpha, coef, bs, nxt=nxt_s)
        if nxt_i is None:
            return None, None
        return u_p.reshape(mp, d), u_s.reshape(bs, d)

    u_p = _modulate_call(xp, ada_p[1], ada_p[0], tlp).reshape(mp, d)
    u_s = _modulate_call(xs, ada_s[1], ada_s[0], bs).reshape(bs, d)
    y_p, y_s = _swiglu(u_p, u_s, w_g1, w_u1, w_d1)
    u_p, u_s = ln_both(y_p, y_s, 2, ln1_g, ln1_b, 0.5, 3)
    widths = (attn_width, attn_width, attn_width, d - attn_width)
    proj = [_project(u_p, u_s, w_in, i * attn_width, w) for i, w in enumerate(widths)]
    attn_p, pool_p = prompt_mixers(*(o[0] for o in proj))
    attn_s, pool_s = sample_mixers(*(o[1] for o in proj))
    y_p, y_s = _wstat_call([attn_p, pool_p], [attn_s, pool_s], [w_out, w_out], [0, 1], 0, d, _sum_epilogue, F32,
                           MIXER_ROWS, MIXER_COLS, "mixer_out")
    u_p, u_s = ln_both(y_p, _rows_major(y_s), 5, ln2_g, ln2_b, 1.0, 6)
    y_p, y_s = _swiglu(u_p, u_s, w_g2, w_u2, w_d2)
    ln_both(y_p, y_s, 8, ln3_g, ln3_b, 0.5, None)
    return xp, xs, proj[1:]


def kernel(x_prompt, x_sample, cache_k, cache_v, state_pool, c_prompt, c_sample,
           w_ada, b_ada, ffn1_w_gate, ffn1_w_up, ffn1_w_down, ln1_g, ln1_b,
           w_in, pool_w, pool_scale, w_out, ln2_g, ln2_b,
           ffn2_w_gate, ffn2_w_up, ffn2_w_down, ln3_g, ln3_b):
    depth = w_ada.shape[0]
    bp, seq, d = x_prompt.shape
    bs, t_new, _ = x_sample.shape
    n_heads, head_dim = cache_k.shape[3], cache_k.shape[4]
    attn_width = n_heads * head_dim
    pool_width = d - attn_width
    assert t_new == 1 and seq == WIN_MAX
    alpha = (2 * depth) ** 0.25
    keep_pool = min(POOL_MAX - 1, seq)

    xp = x_prompt
    xs = x_sample.reshape(1, bs, d)
    rows = bp + bs
    rows_pad = -(-rows // 16) * 16
    c_all = jnp.pad(jnp.concatenate([c_prompt, c_sample], axis=0), ((0, rows_pad - rows), (0, 0)))

    outs = [[] for _ in range(6)]
    for l in range(depth):
        ada = _ada_call(c_all, w_ada[l], b_ada[l])
        ada_p = [ada[:bp, i * d:(i + 1) * d].reshape(bp, 1, d) for i in range(N_ADA)]
        ada_s = [ada[bp:rows, i * d:(i + 1) * d].reshape(1, bs, d) for i in range(N_ADA)]
        weights = (ffn1_w_gate[l], ffn1_w_up[l], ffn1_w_down[l], ln1_g[l], ln1_b[l], w_in[l], w_out[l],
                   ln2_g[l], ln2_b[l], ffn2_w_gate[l], ffn2_w_up[l], ffn2_w_down[l], ln3_g[l], ln3_b[l],
                   alpha, attn_width)

        def prompt_mixers(q, k, v, p):
            attn = _prompt_attn_call(q.reshape(bp, seq, attn_width), k.reshape(bp, seq, attn_width),
                                     v.reshape(bp, seq, attn_width), n_heads, head_dim)
            pool = _prompt_pool_call(p.reshape(bp, seq, pool_width), pool_w[l], pool_scale[l])
            return attn.reshape(bp * seq, attn_width), pool.reshape(bp * seq, pool_width)

        def sample_mixers(q, k, v, p):
            heads = (bs, n_heads, head_dim)
            attn = _sample_attn_call(q.reshape(heads), k.reshape(heads), v.reshape(heads), cache_k[l], cache_v[l])
            pool = _sample_pool_call(jnp.swapaxes(state_pool[l], 0, 1), p, pool_w[l], pool_scale[l])
            return attn.reshape(bs, attn_width), pool

        xp, xs, ((kp, ks), (vp, vs), (pp, ps)) = _layer(xp, xs, ada_p, ada_s, weights, prompt_mixers, sample_mixers)
        outs[0].append(kp.reshape(bp, seq, n_heads, head_dim))
        outs[1].append(vp.reshape(bp, seq, n_heads, head_dim))
        outs[2].append(pp.reshape(bp, seq, pool_width)[:, seq - keep_pool:])
        outs[3].append(ks.reshape(bs, 1, n_heads, head_dim))
        outs[4].append(vs.reshape(bs, 1, n_heads, head_dim))
        outs[5].append(ps.reshape(bs, 1, pool_width))
    return (xp, xs.reshape(bs, 1, d)) + tuple(jnp.stack(o) for o in outs)
```

```python
import functools
import math

import numpy as np
import jax
import jax.numpy as jnp
from jax import lax
from jax.experimental import pallas as pl
from jax.experimental.pallas import tpu as pltpu

F32 = jnp.float32
BF16 = jnp.bfloat16

V7X_SCOPED_VMEM_BYTES = 60000 * 1024
LANE = 128
MXU_DIM = 256

LN_EPS = 1e-5
N_ADA = 9
POOL_WINDOWS = (2, 4, 8, 16)
POOL_MAX = 16
DILATED_PATTERNS = ((128, 1), (512, 4), (2048, 16))
WIN_MAX = 2048
ATTN_BLOCK = 256


def _params(semantics, vmem_bytes):
    limit = int(min(max(vmem_bytes, 16 << 20), V7X_SCOPED_VMEM_BYTES))
    return pltpu.CompilerParams(dimension_semantics=semantics, vmem_limit_bytes=limit)


def _nbytes(shape, dtype):
    return math.prod(shape) * jnp.dtype(dtype).itemsize


def _ada_kernel(c_ref, w_ref, b_ref, o_ref):
    a = jax.nn.silu(c_ref[...]).astype(BF16)
    o_ref[...] = jnp.dot(a, w_ref[...].astype(BF16), preferred_element_type=F32) + b_ref[...]


def _ada_call(c, w, b, tn=512):
    rows, d = c.shape
    n = w.shape[1]
    vmem = 2 * (_nbytes((rows, d), F32) + _nbytes((d, tn), F32) + _nbytes((rows, tn), F32)) \
        + _nbytes((d, tn), BF16) + _nbytes((rows, d), BF16)
    return pl.pallas_call(
        _ada_kernel,
        out_shape=jax.ShapeDtypeStruct((rows, n), F32),
        grid=(n // tn,),
        in_specs=[pl.BlockSpec((rows, d), lambda j: (0, 0)),
                  pl.BlockSpec((d, tn), lambda j: (0, j)),
                  pl.BlockSpec((1, tn), lambda j: (0, j))],
        out_specs=pl.BlockSpec((rows, tn), lambda j: (0, j)),
        compiler_params=_params(("arbitrary",), vmem),
        name="ada_proj",
    )(c, w, b.reshape(1, n))


def _modulate_kernel(x_ref, s_ref, b_ref, u_ref):
    u_ref[...] = (x_ref[...] * (1.0 + s_ref[...]) + b_ref[...]).astype(BF16)


def _ada_spec(la, tl, d):
    if la == 1:
        return pl.BlockSpec((1, 1, d), lambda b, t: (b, 0, 0))
    return pl.BlockSpec((1, tl, d), lambda b, t: (b, t, 0))


def _modulate_call(x, scale, shift, tl):
    bsz, l, d = x.shape
    la = scale.shape[1]
    xs = pl.BlockSpec((1, tl, d), lambda b, t: (b, t, 0))
    vmem = 2 * (_nbytes((tl, d), F32) + _nbytes((tl, d), BF16) + 2 * _nbytes((min(la, tl), d), F32)) \
        + _nbytes((tl, d), F32)
    return pl.pallas_call(
        _modulate_kernel,
        out_shape=jax.ShapeDtypeStruct((bsz, l, d), BF16),
        grid=(bsz, l // tl),
        in_specs=[xs, _ada_spec(la, tl, d), _ada_spec(la, tl, d)],
        out_specs=xs,
        compiler_params=_params(("parallel", "parallel"), vmem),
        name="modulate",
    )(x, scale, shift)


def _ln_kernel(has_next, z_ref, lng_ref, lnb_ref, *rest):
    if has_next:
        s_ref, b_ref, xo_ref, uo_ref = rest
    else:
        (xo_ref,) = rest
    z = z_ref[...]
    mu = jnp.mean(z, axis=-1, keepdims=True)
    zc = z - mu
    var = jnp.mean(zc * zc, axis=-1, keepdims=True)
    xn = zc * lax.rsqrt(var + LN_EPS) * lng_ref[...] + lnb_ref[...]
    xo_ref[...] = xn
    if has_next:
        uo_ref[...] = (xn * (1.0 + s_ref[...]) + b_ref[...]).astype(BF16)


def _ln_call(z, ln_g, ln_b, tl, nxt=None):
    bsz, l, d = z.shape
    xs = pl.BlockSpec((1, tl, d), lambda b, t: (b, t, 0))
    vec = pl.BlockSpec((1, 1, d), lambda b, t: (0, 0, 0))
    ins = [z, ln_g.reshape(1, 1, d), ln_b.reshape(1, 1, d)]
    in_specs = [xs, vec, vec]
    out_shape = [jax.ShapeDtypeStruct((bsz, l, d), F32)]
    out_specs = [xs]
    la = 1
    if nxt is not None:
        la = nxt[0].shape[1]
        ins += list(nxt)
        in_specs += [_ada_spec(la, tl, d), _ada_spec(la, tl, d)]
        out_shape.append(jax.ShapeDtypeStruct((bsz, l, d), BF16))
        out_specs.append(xs)
    vmem = 2 * (2 * _nbytes((tl, d), F32) + _nbytes((tl, d), BF16)
                + 2 * _nbytes((min(la, tl), d), F32)) + 3 * _nbytes((tl, d), F32)
    outs = pl.pallas_call(
        functools.partial(_ln_kernel, nxt is not None),
        out_shape=out_shape,
        grid=(bsz, l // tl),
        in_specs=in_specs,
        out_specs=out_specs,
        compiler_params=_params(("parallel", "parallel"), vmem),
        name="ln_modulate",
    )(*ins)
    return outs if nxt is not None else (outs[0], None)


def _swiglu_epilogue(a_vals, w_vals):
    g = jnp.dot(a_vals[0], w_vals[0], preferred_element_type=F32)
    v = jnp.dot(a_vals[0], w_vals[1], preferred_element_type=F32)
    return jax.nn.silu(g) * v


def _sum_epilogue(a_vals, w_vals):
    acc = None
    for a, w in zip(a_vals, w_vals):
        d = jnp.dot(a, w, preferred_element_type=F32)
        acc = d if acc is None else acc + d
    return acc


def _rstat_kernel(n_a, n_w, epilogue, sub, residual, *refs):
    a_refs = refs[:n_a]
    as_refs = refs[n_a:2 * n_a]
    w_refs = refs[2 * n_a:2 * n_a + n_w]
    pos = 2 * n_a + n_w
    if residual is not None:
        x_ref, g_ref, xs_ref, gs_ref = refs[pos:pos + 4]
        pos += 4
    o_ref, os_ref = refs[pos:]

    def finish(y, x, g):
        if residual is None:
            return y
        alpha, coef = residual
        return alpha * x + (coef * (1.0 + g)) * y

    rows = o_ref.shape[0] // sub
    w_vals = [w_ref[...].astype(BF16) for w_ref in w_refs]
    for r in range(sub):
        part = slice(r * rows, (r + 1) * rows)
        y = epilogue([a_ref[part, :] for a_ref in a_refs], w_vals)
        if residual is not None:
            y = finish(y, x_ref[part, :], g_ref[...])
        o_ref[part, :] = y.astype(o_ref.dtype)

    @pl.when(pl.program_id(0) == 0)
    def _():
        ws = [w_ref[...].astype(BF16) for w_ref in w_refs]
        y = epilogue([r[...] for r in as_refs], ws)
        if residual is not None:
            y = finish(y, xs_ref[...], gs_ref[...])
        os_ref[pl.program_id(1)] = y.astype(os_ref.dtype)


def _rstat_call(a_list, as_list, w_list, w_row_blocks, col_off, n_out, epilogue, out_dtype,
                tm, tn, sub, a_buffers, name, residual=None, res_inputs=()):
    m = a_list[0].shape[0]
    ms = as_list[0].shape[0]
    ks = [a.shape[1] for a in a_list]
    w_ks = [ks[i % len(ks)] for i in range(len(w_list))]
    off = col_off // tn
    mode = pl.Buffered(a_buffers)
    in_specs = [pl.BlockSpec((tm, k), lambda i, n: (i, 0), pipeline_mode=mode) for k in ks]
    in_specs += [pl.BlockSpec((ms, k), lambda i, n: (0, 0)) for k in ks]
    in_specs += [pl.BlockSpec((k, tn), functools.partial(lambda rb, i, n: (rb, off + n), rb))
                 for k, rb in zip(w_ks, w_row_blocks)]
    vmem = sum(a_buffers * _nbytes((tm, k), BF16) + 2 * _nbytes((ms, k), BF16) for k in ks) \
        + sum(2 * _nbytes((k, tn), F32) + _nbytes((k, tn), BF16) for k in w_ks) \
        + 2 * _nbytes((tm, tn), out_dtype) + 2 * _nbytes((ms, n_out), out_dtype) \
        + 6 * _nbytes((tm // sub, tn), F32)
    if residual is not None:
        tiles_per_seq = res_inputs[0].shape[0] // res_inputs[1].shape[0] // tm
        assert tiles_per_seq >= 1
        in_specs += [pl.BlockSpec((tm, tn), lambda i, n: (i, n)),
                     pl.BlockSpec((None, 1, tn), lambda i, n: (i // tiles_per_seq, 0, n)),
                     pl.BlockSpec((ms, tn), lambda i, n: (0, n)),
                     pl.BlockSpec((ms, tn), lambda i, n: (0, n))]
        vmem += 2 * _nbytes((tm + 2 * ms + 8, tn), F32)
    return pl.pallas_call(
        functools.partial(_rstat_kernel, len(a_list), len(w_list), epilogue, sub, residual),
        out_shape=[jax.ShapeDtypeStruct((m, n_out), out_dtype),
                   jax.ShapeDtypeStruct((n_out // tn, ms, tn), out_dtype)],
        grid=(m // tm, n_out // tn),
        in_specs=in_specs,
        out_specs=[pl.BlockSpec((tm, tn), lambda i, n: (i, n)),
                   pl.BlockSpec((n_out // tn, ms, tn), lambda i, n: (0, 0, 0))],
        compiler_params=_params(("arbitrary", "arbitrary"), vmem),
        name=name,
    )(*a_list, *as_list, *w_list, *res_inputs)


def _rows_major(x):
    nb, rows, tn = x.shape
    return jnp.swapaxes(x, 0, 1).reshape(rows, nb * tn)


def _pattern_bias_tiles():
    tb = ATTN_BLOCK
    i = np.arange(tb)[:, None]
    j = np.arange(tb)[None, :]
    tiles = []
    for d in range(4):
        delta = d * tb + i - j
        mult = np.zeros((tb, tb), np.float64)
        for win, dil in DILATED_PATTERNS:
            mult += (delta >= 0) & (delta % dil == 0) & (delta <= win)
        with np.errstate(divide="ignore"):
            tiles.append(np.log(mult))
    assert 3 * tb - (tb - 1) > DILATED_PATTERNS[1][0]
    return np.stack(tiles).astype(np.float32)


def _prompt_attn_kernel(seq, scale, q_ref, k_ref, v_ref, bias_ref, o_ref):
    tb = ATTN_BLOCK
    kb = k_ref[...].astype(BF16)
    vb = v_ref[...].astype(BF16)
    for i in range(seq // tb):
        n = (i + 1) * tb
        qb = q_ref[i * tb:(i + 1) * tb, :].astype(BF16)
        s = lax.dot_general(qb, kb[:n], (((1,), (1,)), ((), ())), preferred_element_type=F32)
        bias = jnp.concatenate([bias_ref[min(i - j, 3)] for j in range(i + 1)], axis=1)
        s = s * scale + bias
        m = jnp.max(s, axis=-1, keepdims=True)
        p = jnp.exp(s - m)
        l = jnp.sum(p, axis=-1, keepdims=True)
        acc = jnp.dot(p.astype(BF16), vb[:n], preferred_element_type=F32)
        o_ref[i * tb:(i + 1) * tb, :] = (acc / l).astype(o_ref.dtype)


def _prompt_attn_call(q, k, v, n_heads, head_dim):
    bsz, seq, width = q.shape
    assert width == n_heads * head_dim and head_dim == LANE and seq % ATTN_BLOCK == 0
    assert seq <= DILATED_PATTERNS[2][0]
    bias = jnp.asarray(_pattern_bias_tiles())
    hs = pl.BlockSpec((None, seq, head_dim), lambda b, h: (b, 0, h))
    vmem = 2 * (3 * _nbytes((seq, head_dim), F32) + _nbytes((seq, head_dim), BF16)
                + _nbytes(bias.shape, F32)) + 2 * _nbytes((seq, head_dim), BF16) \
        + 4 * _nbytes((ATTN_BLOCK, seq), F32)
    return pl.pallas_call(
        functools.partial(_prompt_attn_kernel, seq, head_dim ** -0.5),
        out_shape=jax.ShapeDtypeStruct((bsz, seq, width), BF16),
        grid=(bsz, n_heads),
        in_specs=[hs, hs, hs, pl.BlockSpec(bias.shape, lambda b, h: (0, 0, 0))],
        out_specs=hs,
        compiler_params=_params(("parallel", "parallel"), vmem),
        name="prompt_attention",
    )(q, k, v, bias)


def _prompt_pool_kernel(ts, gdim, x_ref, halo_ref, w_ref, sc_ref, o_ref):
    t = pl.program_id(1)
    halo = jnp.where(t > 0, halo_ref[...], 0.0)
    row = t * ts + lax.broadcasted_iota(jnp.int32, (ts, 1), 0)
    for g, win in enumerate(POOL_WINDOWS):
        lanes = slice(g * gdim, (g + 1) * gdim)
        x = x_ref[:, lanes]
        run = jnp.concatenate([halo[:, lanes], x], axis=0)
        span = 1
        while span < win:
            run = run[span:] + run[:-span]
            span *= 2
        total = run[POOL_MAX - (win - 1):POOL_MAX - (win - 1) + ts]
        cnt = jnp.minimum(row + 1, win).astype(F32)
        d = total / cnt - x
        y = jnp.dot(d.astype(BF16), w_ref[g].astype(BF16), preferred_element_type=F32)
        o_ref[:, lanes] = (y * sc_ref[:, lanes]).astype(o_ref.dtype)


def _prompt_pool_call(p, pool_w, pool_scale, ts=256):
    bsz, seq, width = p.shape
    groups, gdim, _ = pool_w.shape
    hb = ts // POOL_MAX
    vmem = 2 * (_nbytes((ts, width), F32) + _nbytes((POOL_MAX, width), F32) + _nbytes(pool_w.shape, F32)
                + _nbytes((ts, width), BF16)) + 8 * _nbytes((ts + POOL_MAX, gdim), F32)
    return pl.pallas_call(
        functools.partial(_prompt_pool_kernel, ts, gdim),
        out_shape=jax.ShapeDtypeStruct((bsz, seq, width), BF16),
        grid=(bsz, seq // ts),
        in_specs=[pl.BlockSpec((None, ts, width), lambda b, t: (b, t, 0)),
                  pl.BlockSpec((None, POOL_MAX, width), lambda b, t: (b, jnp.maximum(t * hb - 1, 0), 0)),
                  pl.BlockSpec(pool_w.shape, lambda b, t: (0, 0, 0)),
                  pl.BlockSpec((1, width), lambda b, t: (0, 0))],
        out_specs=pl.BlockSpec((None, ts, width), lambda b, t: (b, t, 0)),
        compiler_params=_params(("parallel", "arbitrary"), vmem),
        name="prompt_pool",
    )(p, p, pool_w, pool_scale.reshape(1, width))


def _bf16_round(x):
    return x.astype(BF16).astype(F32)


def _sample_attn_kernel(scale, q_ref, kn_ref, vn_ref, *rest):
    k_refs, v_refs, o_ref = rest[0:3], rest[3:6], rest[6]
    q = _bf16_round(q_ref[...])[None]
    kn = _bf16_round(kn_ref[...])[None]
    vn = _bf16_round(vn_ref[...])[None]
    s_new = jnp.sum(q * kn, axis=-1, keepdims=True) * scale
    outs, lses = [], []
    for k_ref, v_ref in zip(k_refs, v_refs):
        s = jnp.sum(q * _bf16_round(k_ref[...]), axis=-1, keepdims=True) * scale
        m = jnp.maximum(jnp.max(s, axis=0, keepdims=True), s_new)
        e = jnp.exp(s - m)
        e_new = jnp.exp(s_new - m)
        l = jnp.sum(e, axis=0, keepdims=True) + e_new
        o = jnp.sum(_bf16_round(e / l) * _bf16_round(v_ref[...]), axis=0, keepdims=True)
        outs.append(o + _bf16_round(e_new / l) * vn)
        lses.append(m + jnp.log(l))
    top = functools.reduce(jnp.maximum, lses)
    ws = [jnp.exp(x - top) for x in lses]
    den = functools.reduce(lambda a, b: a + b, ws)
    mix = functools.reduce(lambda a, b: a + b, [(w / den) * o for w, o in zip(ws, outs)])
    o_ref[...] = mix[0].astype(o_ref.dtype)


def _sample_attn_call(q, k_new, v_new, cache_k, cache_v):
    bsz, wb, n_heads, head_dim = cache_k.shape
    rows = DILATED_PATTERNS[0][0]
    assert wb == WIN_MAX and all(win // dil == rows for win, dil in DILATED_PATTERNS)
    row_spec = pl.BlockSpec((None, n_heads, head_dim), lambda b: (b, 0, 0))
    views, specs = [], []
    for cache in (cache_k, cache_v):
        for win, dil in DILATED_PATTERNS:
            views.append(cache.reshape(bsz, wb // dil, dil, n_heads, head_dim))
            blk = wb // dil // rows - 1
            specs.append(pl.BlockSpec((None, rows, None, n_heads, head_dim),
                                      functools.partial(lambda blk, b: (b, blk, 0, 0, 0), blk)))
    vmem = 2 * 6 * _nbytes((rows, n_heads, head_dim), F32) + 6 * _nbytes((rows, n_heads, head_dim), F32) \
        + 12 * _nbytes((rows, n_heads, LANE), F32)
    return pl.pallas_call(
        functools.partial(_sample_attn_kernel, head_dim ** -0.5),
        out_shape=jax.ShapeDtypeStruct((bsz, n_heads, head_dim), BF16),
        grid=(bsz,),
        in_specs=[row_spec, row_spec, row_spec] + specs,
        out_specs=row_spec,
        compiler_params=_params(("parallel",), vmem),
        name="sample_attention",
    )(q, k_new, v_new, *views)


def _sample_pool_kernel(gdim, st_ref, x_ref, w_ref, sc_ref, o_ref):
    nbuf = st_ref.shape[0]
    for g, win in enumerate(POOL_WINDOWS):
        lanes = slice(g * gdim, (g + 1) * gdim)
        x = x_ref[:, lanes]
        total = x
        for back in range(1, win):
            total = total + st_ref[nbuf - back, :, lanes]
        d = total / float(win) - x
        y = jnp.dot(d.astype(BF16), w_ref[g].astype(BF16), preferred_element_type=F32)
        o_ref[:, lanes] = (y * sc_ref[:, lanes]).astype(o_ref.dtype)


def _sample_pool_call(state, x, pool_w, pool_scale):
    nbuf, bsz, width = state.shape
    assert nbuf == POOL_MAX - 1
    groups, gdim, _ = pool_w.shape
    full = lambda shape: pl.BlockSpec(shape, lambda i: (0,) * len(shape))
    vmem = 2 * (_nbytes((POOL_MAX, bsz, width), F32) + 2 * _nbytes((bsz, width), F32)
                + _nbytes(pool_w.shape, F32)) + _nbytes(pool_w.shape, BF16)
    return pl.pallas_call(
        functools.partial(_sample_pool_kernel, gdim),
        out_shape=jax.ShapeDtypeStruct((bsz, width), BF16),
        grid=(1,),
        in_specs=[full(state.shape), full(x.shape), full(pool_w.shape), full((1, width))],
        out_specs=full((bsz, width)),
        compiler_params=_params(("arbitrary",), vmem),
        name="sample_pool",
    )(state, x, pool_w, pool_scale.reshape(1, width))


PROMPT_LN_ROWS = 256
GATE_UP_ROWS = 2048
DOWN_ROWS = 1024
MIXER_ROWS = 2048


def _swiglu(u_p, u_s, w_gate, w_up, w_down, residual, res_inputs):
    d_ff = w_gate.shape[1]
    d = w_down.shape[1]
    h_p, h_s = _rstat_call([u_p], [u_s], [w_gate, w_up], [0, 0], 0, d_ff, _swiglu_epilogue, BF16,
                           GATE_UP_ROWS, MXU_DIM, 4, 1, "swiglu_gate_up")
    z_p, z_s = _rstat_call([h_p], [_rows_major(h_s)], [w_down], [0], 0, d, _sum_epilogue, F32,
                           DOWN_ROWS, MXU_DIM, 1, 1, "swiglu_down", residual, res_inputs)
    return z_p, _rows_major(z_s)


def _project(u_p, u_s, w_in, col_off, width):
    o_p, o_s = _rstat_call([u_p], [u_s], [w_in], [0], col_off, width, _sum_epilogue, F32,
                           MIXER_ROWS, MXU_DIM, 4, 2, "mixer_in")
    return o_p, _rows_major(o_s)


def _layer(xp, xs, ada_p, ada_s, weights, prompt_mixers, sample_mixers):
    (w_g1, w_u1, w_d1, ln1_g, ln1_b, w_in, w_out, ln2_g, ln2_b, w_g2, w_u2, w_d2, ln3_g, ln3_b,
     alpha, attn_width) = weights
    bp, seq, d = xp.shape
    bs = xs.shape[1]
    mp = bp * seq
    tlp = PROMPT_LN_ROWS

    def res_inputs(gate_i):
        return (xp.reshape(mp, d), ada_p[gate_i], xs.reshape(bs, d), ada_s[gate_i].reshape(bs, d))

    def ln_both(z_p, z_s, ln_g, ln_b, nxt_i):
        nonlocal xp, xs
        nxt_p = None if nxt_i is None else (ada_p[nxt_i + 1], ada_p[nxt_i])
        nxt_s = None if nxt_i is None else (ada_s[nxt_i + 1], ada_s[nxt_i])
        xp, u_p = _ln_call(z_p.reshape(bp, seq, d), ln_g, ln_b, tlp, nxt=nxt_p)
        xs, u_s = _ln_call(z_s.reshape(1, bs, d), ln_g, ln_b, bs, nxt=nxt_s)
        if nxt_i is None:
            return None, None
        return u_p.reshape(mp, d), u_s.reshape(bs, d)

    u_p = _modulate_call(xp, ada_p[1], ada_p[0], tlp).reshape(mp, d)
    u_s = _modulate_call(xs, ada_s[1], ada_s[0], bs).reshape(bs, d)
    z_p, z_s = _swiglu(u_p, u_s, w_g1, w_u1, w_d1, (alpha, 0.5), res_inputs(2))
    u_p, u_s = ln_both(z_p, z_s, ln1_g, ln1_b, 3)
    widths = (attn_width, attn_width, attn_width, d - attn_width)
    proj = [_project(u_p, u_s, w_in, i * attn_width, w) for i, w in enumerate(widths)]
    attn_p, pool_p = prompt_mixers(*(o[0] for o in proj))
    attn_s, pool_s = sample_mixers(*(o[1] for o in proj))
    z_p, z_s = _rstat_call([attn_p, pool_p], [attn_s, pool_s], [w_out, w_out], [0, 1], 0, d, _sum_epilogue, F32,
                           MIXER_ROWS, MXU_DIM, 4, 2, "mixer_out", (alpha, 1.0), res_inputs(5))
    u_p, u_s = ln_both(z_p, _rows_major(z_s), ln2_g, ln2_b, 6)
    z_p, z_s = _swiglu(u_p, u_s, w_g2, w_u2, w_d2, (alpha, 0.5), res_inputs(8))
    ln_both(z_p, z_s, ln3_g, ln3_b, None)
    return xp, xs, proj[1:]


def kernel(x_prompt, x_sample, cache_k, cache_v, state_pool, c_prompt, c_sample,
           w_ada, b_ada, ffn1_w_gate, ffn1_w_up, ffn1_w_down, ln1_g, ln1_b,
           w_in, pool_w, pool_scale, w_out, ln2_g, ln2_b,
           ffn2_w_gate, ffn2_w_up, ffn2_w_down, ln3_g, ln3_b):
    depth = w_ada.shape[0]
    bp, seq, d = x_prompt.shape
    bs, t_new, _ = x_sample.shape
    n_heads, head_dim = cache_k.shape[3], cache_k.shape[4]
    attn_width = n_heads * head_dim
    pool_width = d - attn_width
    assert t_new == 1 and seq == WIN_MAX
    alpha = (2 * depth) ** 0.25
    keep_pool = min(POOL_MAX - 1, seq)

    xp = x_prompt
    xs = x_sample.reshape(1, bs, d)
    rows = bp + bs
    rows_pad = -(-rows // 16) * 16
    c_all = jnp.pad(jnp.concatenate([c_prompt, c_sample], axis=0), ((0, rows_pad - rows), (0, 0)))

    outs = [[] for _ in range(6)]
    for l in range(depth):
        ada = _ada_call(c_all, w_ada[l], b_ada[l])
        ada_p = [ada[:bp, i * d:(i + 1) * d].reshape(bp, 1, d) for i in range(N_ADA)]
        ada_s = [ada[bp:rows, i * d:(i + 1) * d].reshape(1, bs, d) for i in range(N_ADA)]
        weights = (ffn1_w_gate[l], ffn1_w_up[l], ffn1_w_down[l], ln1_g[l], ln1_b[l], w_in[l], w_out[l],
                   ln2_g[l], ln2_b[l], ffn2_w_gate[l], ffn2_w_up[l], ffn2_w_down[l], ln3_g[l], ln3_b[l],
                   alpha, attn_width)

        def prompt_mixers(q, k, v, p):
            attn = _prompt_attn_call(q.reshape(bp, seq, attn_width), k.reshape(bp, seq, attn_width),
                                     v.reshape(bp, seq, attn_width), n_heads, head_dim)
            pool = _prompt_pool_call(p.reshape(bp, seq, pool_width), pool_w[l], pool_scale[l])
            return attn.reshape(bp * seq, attn_width), pool.reshape(bp * seq, pool_width)

        def sample_mixers(q, k, v, p):
            heads = (bs, n_heads, head_dim)
            attn = _sample_attn_call(q.reshape(heads), k.reshape(heads), v.reshape(heads), cache_k[l], cache_v[l])
            pool = _sample_pool_call(jnp.swapaxes(state_pool[l], 0, 1), p, pool_w[l], pool_scale[l])
            return attn.reshape(bs, attn_width), pool

        xp, xs, ((kp, ks), (vp, vs), (pp, ps)) = _layer(xp, xs, ada_p, ada_s, weights, prompt_mixers, sample_mixers)
        outs[0].append(kp.reshape(bp, seq, n_heads, head_dim))
        outs[1].append(vp.reshape(bp, seq, n_heads, head_dim))
        outs[2].append(pp.reshape(bp, seq, pool_width)[:, seq - keep_pool:])
        outs[3].append(ks.reshape(bs, 1, n_heads, head_dim))
        outs[4].append(vs.reshape(bs, 1, n_heads, head_dim))
        outs[5].append(ps.reshape(bs, 1, pool_width))
    return (xp, xs.reshape(bs, 1, d)) + tuple(jnp.stack(o) for o in outs)
```
